```python
import math
import jax, jax.numpy as jnp
from jax import lax
import numpy as np

D_MODEL = 1024
BATCH = 4
SEQ = 8192
DEPTH = 1

N_META = 16
EPS = 1e-6
MLA_HEADS = 8
Q_LORA = 384
KV_LORA = 256
QK_NOPE = 64
QK_ROPE = 32
V_DIM = 64
ROPE_BASE = 10000.0
Q_BLOCK = 128
LRU_WIDTH = D_MODEL
LRU_BLOCKS = 8
LRU_BW = LRU_WIDTH // LRU_BLOCKS
CONV_WIDTH = 4
LRU_C = 8.0
N_EXPERTS = 32
TOP_K = 4
D_FF = D_MODEL
SWIGLU_ALPHA = 1.702
SWIGLU_LIMIT = 7.0
EXPERT_BLOCK = 128
OFF_CQ = 0
OFF_CKV = OFF_CQ + Q_LORA
OFF_KR = OFF_CKV + KV_LORA
OFF_RX = OFF_KR + QK_ROPE
OFF_RG = OFF_RX + LRU_WIDTH
OFF_GATE = OFF_RG + LRU_WIDTH
D_IN = OFF_GATE + 2 * D_MODEL

kernel_name = 'hybrid_mla_rglru_moe_meta'


def rms_norm(x, g):
    xf = x.astype(jnp.float32)
    var = jnp.mean(xf * xf, axis=-1, keepdims=True)
    return (xf * lax.rsqrt(var + EPS)).astype(x.dtype) * g


def rope_tables(T, dtype):
    inv = ROPE_BASE ** (-jnp.arange(0, QK_ROPE, 2, dtype=jnp.float32) / QK_ROPE)
    ang = jnp.arange(T, dtype=jnp.float32)[:, None] * inv[None, :]
    return jnp.cos(ang).astype(dtype), jnp.sin(ang).astype(dtype)


def apply_rope(x, cos, sin):
    x1, x2 = jnp.split(x, 2, axis=-1)
    return jnp.concatenate([x1 * cos - x2 * sin, x1 * sin + x2 * cos], axis=-1)


def causal_block_attention(q_nope, q_pe, k_nope, k_pe, v):
    B, T, H, _ = q_nope.shape
    n_blk = -(-T // Q_BLOCK)
    Tp = n_blk * Q_BLOCK
    pad = Tp - T

    def padt(a):
        return jnp.pad(a, [(0, 0), (0, pad)] + [(0, 0)] * (a.ndim - 2))

    q_nope, q_pe, k_nope, k_pe, v = padt(q_nope), padt(q_pe), padt(k_nope), padt(k_pe), padt(v)
    scale = (QK_NOPE + QK_ROPE) ** -0.5
    k_pos = jnp.arange(Tp)

    def one_block(i):
        s = i * Q_BLOCK
        qn = lax.dynamic_slice_in_dim(q_nope, s, Q_BLOCK, axis=1)
        qp = lax.dynamic_slice_in_dim(q_pe, s, Q_BLOCK, axis=1)
        scores = (jnp.einsum('bqhd,bkhd->bhqk', qn, k_nope, preferred_element_type=jnp.float32)
                  + jnp.einsum('bqhr,bkr->bhqk', qp, k_pe, preferred_element_type=jnp.float32))
        q_pos = s + jnp.arange(Q_BLOCK)
        mask = k_pos[None, :] <= q_pos[:, None]
        scores = jnp.where(mask, scores * scale, -jnp.inf)
        p = jax.nn.softmax(scores, axis=-1).astype(v.dtype)
        return jnp.einsum('bhqk,bkhd->bqhd', p, v)

    out = lax.map(one_block, jnp.arange(n_blk))
    out = jnp.moveaxis(out, 0, 1).reshape(B, Tp, H, V_DIM)
    return out[:, :T]


def mla_branch(p_cq, p_ckv, p_kr, q_norm_g, w_uq, kv_norm_g, w_ukv, w_ao, cos, sin):
    B, T, _ = p_cq.shape
    c_q = rms_norm(p_cq, q_norm_g)
    q = (c_q @ w_uq).reshape(B, T, MLA_HEADS, QK_NOPE + QK_ROPE)
    q_nope = q[..., :QK_NOPE]
    q_pe = apply_rope(q[..., QK_NOPE:], cos[:, None, :], sin[:, None, :])
    c_kv = rms_norm(p_ckv, kv_norm_g)
    kv = (c_kv @ w_ukv).reshape(B, T, MLA_HEADS, QK_NOPE + V_DIM)
    k_nope, v = kv[..., :QK_NOPE], kv[..., QK_NOPE:]
    k_pe = apply_rope(p_kr, cos, sin)
    o = causal_block_attention(q_nope, q_pe, k_nope, k_pe, v)
    return o.reshape(B, T, MLA_HEADS * V_DIM) @ w_ao


def rglru_branch(p_rx, p_rg, conv_w, conv_b, w_rg_a, b_rg_a, w_rg_x, b_rg_x, rg_lambda, w_ro):
    B, T, W = p_rx.shape
    xc = lax.conv_general_dilated(p_rx, conv_w[:, None, :], window_strides=(1,),
                                  padding=[(CONV_WIDTH - 1, 0)],
                                  dimension_numbers=('NWC', 'WIO', 'NWC'),
                                  feature_group_count=W) + conv_b
    xb = xc.reshape(B, T, LRU_BLOCKS, LRU_BW)
    r = jax.nn.sigmoid(jnp.einsum('btni,nij->btnj', xb, w_rg_a).reshape(B, T, W) + b_rg_a)
    ig = jax.nn.sigmoid(jnp.einsum('btni,nij->btnj', xb, w_rg_x).reshape(B, T, W) + b_rg_x)
    log_a = (-LRU_C * r.astype(jnp.float32)) * jax.nn.softplus(-rg_lambda.astype(jnp.float32))
    a = jnp.exp(log_a)
    b = jnp.sqrt(-jnp.expm1(2.0 * log_a)) * (ig * xc).astype(jnp.float32)

    def combine(left, right):
        a1, b1 = left
        a2, b2 = right
        return a1 * a2, a2 * b1 + b2

    _, h = lax.associative_scan(combine, (a, b), axis=1)
    y = h.astype(p_rx.dtype) * jax.nn.gelu(p_rg)
    return y @ w_ro


def moe_ffn(xn, w_router, b_router, w_up, b_up, w_down, b_down):
    B, T, D = xn.shape
    xt = xn.reshape(-1, D)
    N = xt.shape[0]
    logits = (xt @ w_router + b_router).astype(jnp.float32)
    top_val, top_idx = lax.top_k(logits, TOP_K)
    gates = jax.nn.softmax(top_val, axis=-1).astype(xn.dtype)
    NK = N * TOP_K
    e_flat = top_idx.reshape(-1)
    order = jnp.argsort(e_flat)
    e_sorted = e_flat[order]
    tok_sorted = order // TOP_K
    counts = jnp.bincount(e_flat, length=N_EXPERTS)
    padded = (counts + EXPERT_BLOCK - 1) // EXPERT_BLOCK * EXPERT_BLOCK
    start = jnp.cumsum(counts) - counts
    pend = jnp.cumsum(padded)
    pstart = pend - padded
    dest = pstart[e_sorted] + (jnp.arange(NK) - start[e_sorted])
    n_blocks = -(-NK // EXPERT_BLOCK) + N_EXPERTS
    P = n_blocks * EXPERT_BLOCK
    x_pad = jnp.zeros((P, D), xt.dtype).at[dest].set(xt[tok_sorted])
    block_e = jnp.minimum(jnp.searchsorted(pend, jnp.arange(n_blocks) * EXPERT_BLOCK, side='right'),
                          N_EXPERTS - 1)

    def expert_block(args):
        xb, e = args
        hdn = xb @ w_up[e] + b_up[e]
        x_glu = jnp.minimum(hdn[:, :D_FF], SWIGLU_LIMIT)
        x_lin = jnp.clip(hdn[:, D_FF:], -SWIGLU_LIMIT, SWIGLU_LIMIT)
        act = x_glu * jax.nn.sigmoid(SWIGLU_ALPHA * x_glu) * (x_lin + 1.0)
        return act @ w_down[e] + b_down[e]

    y_pad = lax.map(expert_block, (x_pad.reshape(n_blocks, EXPERT_BLOCK, D), block_e))
    y_sorted = y_pad.reshape(P, D)[dest]
    g_sorted = gates.reshape(-1)[order]
    y = jnp.zeros_like(xt).at[tok_sorted].add(y_sorted * g_sorted[:, None])
    return y.reshape(B, T, D)


def setup_inputs(seed: int = 0) -> dict:
    key = jax.random.key(seed)
    ks = jax.random.split(key, 32)
    f32 = jnp.float32

    def nrm(k, shape, scale):
        return jax.random.normal(k, shape, f32) * scale

    def gain(k, shape):
        return 1.0 + 0.01 * jax.random.normal(k, shape, f32)

    L = DEPTH
    a0 = jax.random.uniform(ks[13], (L, LRU_WIDTH), f32, 0.9, 0.999)
    return {
        'x': nrm(ks[0], (BATCH, SEQ, D_MODEL), 1.0),
        'meta_tokens': nrm(ks[1], (N_META, D_MODEL), 1.0),
        'norm_mix_g': gain(ks[2], (L, D_MODEL)),
        'w_in': nrm(ks[3], (L, D_MODEL, D_IN), D_MODEL ** -0.5),
        'b_gate': nrm(ks[4], (L, 2 * D_MODEL), 0.02),
        'q_norm_g': gain(ks[5], (L, Q_LORA)),
        'w_uq': nrm(ks[6], (L, Q_LORA, MLA_HEADS * (QK_NOPE + QK_ROPE)), Q_LORA ** -0.5),
        'kv_norm_g': gain(ks[7], (L, KV_LORA)),
        'w_ukv': nrm(ks[8], (L, KV_LORA, MLA_HEADS * (QK_NOPE + V_DIM)), KV_LORA ** -0.5),
        'w_ao': nrm(ks[9], (L, MLA_HEADS * V_DIM, D_MODEL), (MLA_HEADS * V_DIM) ** -0.5),
        'conv_w': nrm(ks[10], (L, CONV_WIDTH, LRU_WIDTH), CONV_WIDTH ** -0.5),
        'conv_b': nrm(ks[11], (L, LRU_WIDTH), 0.02),
        'w_rg_a': nrm(ks[12], (L, LRU_BLOCKS, LRU_BW, LRU_BW), LRU_BW ** -0.5),
        'b_rg_a': nrm(ks[14], (L, LRU_WIDTH), 0.02),
        'w_rg_x': nrm(ks[15], (L, LRU_BLOCKS, LRU_BW, LRU_BW), LRU_BW ** -0.5),
        'b_rg_x': nrm(ks[16], (L, LRU_WIDTH), 0.02),
        'rg_lambda': jnp.log(a0) - jnp.log1p(-a0),
        'w_ro': nrm(ks[17], (L, LRU_WIDTH, D_MODEL), LRU_WIDTH ** -0.5),
        'w_out': nrm(ks[18], (L, D_MODEL, D_MODEL), D_MODEL ** -0.5),
        'norm_ffn_g': gain(ks[19], (L, D_MODEL)),
        'w_router': nrm(ks[20], (L, D_MODEL, N_EXPERTS), D_MODEL ** -0.5),
        'b_router': nrm(ks[21], (L, N_EXPERTS), 0.01),
        'w_up': nrm(ks[22], (L, N_EXPERTS, D_MODEL, 2 * D_FF), D_MODEL ** -0.5),
        'b_up': nrm(ks[23], (L, N_EXPERTS, 2 * D_FF), 0.02),
        'w_down': nrm(ks[24], (L, N_EXPERTS, D_FF, D_MODEL), D_FF ** -0.5),
        'b_down': nrm(ks[25], (L, N_EXPERTS, D_MODEL), 0.02),
        'norm_final_g': gain(ks[26], (D_MODEL,)),
    }


def reference(x, meta_tokens, norm_mix_g, w_in, b_gate, q_norm_g, w_uq, kv_norm_g, w_ukv, w_ao,
              conv_w, conv_b, w_rg_a, b_rg_a, w_rg_x, b_rg_x, rg_lambda, w_ro, w_out,
              norm_ffn_g, w_router, b_router, w_up, b_up, w_down, b_down, norm_final_g):
    B = x.shape[0]
    meta = jnp.broadcast_to(meta_tokens[None].astype(x.dtype), (B, N_META, D_MODEL))
    h = jnp.concatenate([meta, x], axis=1)
    T = h.shape[1]
    cos, sin = rope_tables(T, h.dtype)
    for l in range(DEPTH):
        xn = rms_norm(h, norm_mix_g[l])
        proj = xn @ w_in[l]
        attn = mla_branch(proj[..., OFF_CQ:OFF_CKV], proj[..., OFF_CKV:OFF_KR], proj[..., OFF_KR:OFF_RX],
                          q_norm_g[l], w_uq[l], kv_norm_g[l], w_ukv[l], w_ao[l], cos, sin)
        rec = rglru_branch(proj[..., OFF_RX:OFF_RG], proj[..., OFF_RG:OFF_GATE], conv_w[l], conv_b[l],
                           w_rg_a[l], b_rg_a[l], w_rg_x[l], b_rg_x[l], rg_lambda[l], w_ro[l])
        g = jax.nn.sigmoid(proj[..., OFF_GATE:] + b_gate[l])
        merged = g[..., :D_MODEL] * attn + g[..., D_MODEL:] * rec
        h = h + merged @ w_out[l]
        h = h + moe_ffn(rms_norm(h, norm_ffn_g[l]), w_router[l], b_router[l],
                        w_up[l], b_up[l], w_down[l], b_down[l])
    h = rms_norm(h, norm_final_g)
    return h[:, N_META:]
```

```python
import functools
import math

import jax
import jax.numpy as jnp
from jax import lax
from jax.experimental import pallas as pl
from jax.experimental.pallas import tpu as pltpu

D_MODEL = 1024
N_META = 16
EPS = 1e-6
MLA_HEADS = 8
Q_LORA = 384
KV_LORA = 256
QK_NOPE = 64
QK_ROPE = 32
V_DIM = 64
ROPE_BASE = 10000.0
LRU_WIDTH = D_MODEL
LRU_BLOCKS = 8
LRU_BW = LRU_WIDTH // LRU_BLOCKS
CONV_WIDTH = 4
LRU_C = 8.0
N_EXPERTS = 32
TOP_K = 4
D_FF = D_MODEL
SWIGLU_ALPHA = 1.702
SWIGLU_LIMIT = 7.0
OFF_CKV = Q_LORA
OFF_KR = OFF_CKV + KV_LORA
OFF_RX = OFF_KR + QK_ROPE
OFF_RG = OFF_RX + LRU_WIDTH
OFF_GATE = OFF_RG + LRU_WIDTH

LANES = 128
SUBLANES = 8
HEAD_PAD = LANES
HEAD_GROUP = 4
V_PAD = HEAD_GROUP * V_DIM
N_GROUPS = MLA_HEADS // HEAD_GROUP
ROUTER_PAD = LANES
HALF = D_MODEL // 2

T_ALIGN = 640
TILE_PROJ = 640
TILE_ATT = 640
TILE_LRU = 320
TILE_MERGE = 512
TILE_DISPATCH = 512
TILE_EXPERT = 256
TILE_COMBINE = 256
VMEM_LIMIT = 56 * 1024 * 1024

F32 = jnp.float32
BF16 = jnp.bfloat16


def _rms(x, g):
    var = jnp.mean(x * x, axis=-1, keepdims=True)
    return x * lax.rsqrt(var + EPS) * g


def _dot(a, b):
    return jnp.dot(a, b, preferred_element_type=F32)


def _const_spec(shape):
    n = len(shape)
    return pl.BlockSpec(shape, lambda *_: (0,) * n)


def _mla_proj_kernel(h_ref, g_ref, wcq_ref, wckv_ref, wkr_ref, gq_ref, gkv_ref, wqa_ref, wqb_ref,
                     wkn_ref, wvx_ref, cq_ref, sq_ref, ck_ref, sk_ref, q_ref, k_ref, v_ref):
    xn = _rms(h_ref[...], g_ref[...]).astype(BF16)
    c_q = _rms(_dot(xn, wcq_ref[...]), gq_ref[...]).astype(BF16)
    c_kv = _rms(_dot(xn, wckv_ref[...]), gkv_ref[...]).astype(BF16)
    kr = _dot(xn, wkr_ref[...])
    k_pe = kr[:, :HEAD_PAD] * ck_ref[...] + kr[:, HEAD_PAD:] * sk_ref[...]
    qa = _dot(c_q, wqa_ref[...])
    qb = _dot(c_q, wqb_ref[...])
    kn = _dot(c_kv, wkn_ref[...])
    vx = _dot(c_kv, wvx_ref[...])
    cq = cq_ref[...]
    sq = sq_ref[...]
    for h in range(MLA_HEADS):
        lo = h * HEAD_PAD
        q_ref[h] = (qa[:, lo:lo + HEAD_PAD] * cq + qb[:, lo:lo + HEAD_PAD] * sq).astype(BF16)
        k_ref[h] = (kn[:, lo:lo + HEAD_PAD] + k_pe).astype(BF16)
        v_ref[h] = vx[:, h * V_PAD:(h + 1) * V_PAD].astype(BF16)


def _mla_proj(h0, g, wcq, wckv, wkr, gq, gkv, wqa, wqb, wkn, wvx, cq, sq, ck, sk):
    B, TP, D = h0.shape
    nt = TP // TILE_PROJ
    tab = pl.BlockSpec((TILE_PROJ, HEAD_PAD), lambda b, i: (i, 0))
    consts = [g, wcq, wckv, wkr, gq, gkv, wqa, wqb, wkn, wvx]
    return pl.pallas_call(
        _mla_proj_kernel,
        grid=(B, nt),
        in_specs=[pl.BlockSpec((None, TILE_PROJ, D), lambda b, i: (b, i, 0))]
        + [_const_spec(c.shape) for c in consts] + [tab, tab, tab, tab],
        out_specs=[
            pl.BlockSpec((None, MLA_HEADS, TILE_PROJ, HEAD_PAD), lambda b, i: (b, 0, i, 0)),
            pl.BlockSpec((None, MLA_HEADS, TILE_PROJ, HEAD_PAD), lambda b, i: (b, 0, i, 0)),
            pl.BlockSpec((None, MLA_HEADS, TILE_PROJ, V_PAD), lambda b, i: (b, 0, i, 0)),
        ],
        out_shape=[
            jax.ShapeDtypeStruct((B, MLA_HEADS, TP, HEAD_PAD), BF16),
            jax.ShapeDtypeStruct((B, MLA_HEADS, TP, HEAD_PAD), BF16),
            jax.ShapeDtypeStruct((B, MLA_HEADS, TP, V_PAD), BF16),
        ],
        compiler_params=pltpu.CompilerParams(
            dimension_semantics=("parallel", "parallel"), vmem_limit_bytes=VMEM_LIMIT),
        name="mla_proj",
    )(h0, *consts, cq, sq, ck, sk)


def _attention_kernel(q_ref, k_ref, v_ref, o_ref, m_ref, l_ref, acc_ref):
    qi = pl.program_id(2)
    ki = pl.program_id(3)
    tq = q_ref.shape[1]
    tk = k_ref.shape[1]

    @pl.when(ki == 0)
    def _():
        m_ref[...] = jnp.full(m_ref.shape, -jnp.inf, F32)
        l_ref[...] = jnp.zeros(l_ref.shape, F32)
        acc_ref[...] = jnp.zeros(acc_ref.shape, F32)

    def lane_groups(cols):
        lane = lax.broadcasted_iota(jnp.int32, (tq, V_PAD), 1)
        out = jnp.broadcast_to(cols[HEAD_GROUP - 1], (tq, V_PAD))
        for j in range(HEAD_GROUP - 2, -1, -1):
            out = jnp.where(lane < (j + 1) * V_DIM, cols[j], out)
        return out

    def step(masked):
        ps = []
        alphas = []
        for j in range(HEAD_GROUP):
            s = lax.dot_general(q_ref[j], k_ref[j], (((1,), (1,)), ((), ())),
                                preferred_element_type=F32)
            if masked:
                row = lax.broadcasted_iota(jnp.int32, (tq, tk), 0)
                col = lax.broadcasted_iota(jnp.int32, (tq, tk), 1)
                s = jnp.where(col <= row, s, -jnp.inf)
            m_prev = m_ref[j]
            m_new = jnp.maximum(m_prev, jnp.max(s, axis=1, keepdims=True))
            alpha = jnp.exp(m_prev - m_new)
            p = jnp.exp(s - m_new)
            l_ref[j] = alpha * l_ref[j] + jnp.sum(p, axis=1, keepdims=True)
            m_ref[j] = m_new
            ps.append(p.astype(BF16))
            alphas.append(alpha)
        p_cat = jnp.concatenate(ps, axis=1)
        v_cat = v_ref[...].reshape(HEAD_GROUP * tk, V_PAD)
        acc_ref[...] = acc_ref[...] * lane_groups(alphas) + _dot(p_cat, v_cat)

    @pl.when(ki < qi)
    def _():
        step(False)

    @pl.when(ki == qi)
    def _():
        step(True)
        inv = lane_groups([1.0 / l_ref[j] for j in range(HEAD_GROUP)])
        o_ref[...] = (acc_ref[...] * inv).astype(o_ref.dtype)


def _attention(q, k, v):
    B, H, TP, _ = q.shape
    nt = TP // TILE_ATT
    kv_map = lambda b, g, qi, ki: (b, g, jnp.minimum(ki, qi), 0)
    return pl.pallas_call(
        _attention_kernel,
        grid=(B, N_GROUPS, nt, nt),
        in_specs=[
            pl.BlockSpec((None, HEAD_GROUP, TILE_ATT, HEAD_PAD), lambda b, g, qi, ki: (b, g, qi, 0)),
            pl.BlockSpec((None, HEAD_GROUP, TILE_ATT, HEAD_PAD), kv_map),
            pl.BlockSpec((None, HEAD_GROUP, TILE_ATT, V_PAD), kv_map),
        ],
        out_specs=pl.BlockSpec((None, TILE_ATT, V_PAD), lambda b, g, qi, ki: (b, qi, g)),
        out_shape=jax.ShapeDtypeStruct((B, TP, MLA_HEADS * V_DIM), BF16),
        scratch_shapes=[
            pltpu.VMEM((HEAD_GROUP, TILE_ATT, 1), F32),
            pltpu.VMEM((HEAD_GROUP, TILE_ATT, 1), F32),
            pltpu.VMEM((TILE_ATT, V_PAD), F32),
        ],
        compiler_params=pltpu.CompilerParams(
            dimension_semantics=("parallel", "parallel", "parallel", "arbitrary"),
            vmem_limit_bytes=VMEM_LIMIT),
        name="attention",
    )(q, k, v)


def _rglru_kernel(h_ref, g_ref, wrx_ref, wrg_ref, cw_ref, cb_ref, wa_ref, ba_ref, wx_ref, bx_ref,
                  lam_ref, y_ref, carry_ref, tail_ref, a_ref, b_ref, hs_ref):
    t = pl.program_id(1)
    tt = h_ref.shape[0]
    ng = tt // SUBLANES

    @pl.when(t == 0)
    def _():
        carry_ref[...] = jnp.zeros(carry_ref.shape, F32)
        tail_ref[...] = jnp.zeros(tail_ref.shape, F32)

    xn = _rms(h_ref[...], g_ref[...]).astype(BF16)
    rx = _dot(xn, wrx_ref[...])
    rg = _dot(xn, wrg_ref[...])

    ext = jnp.concatenate([tail_ref[...], rx], axis=0)
    tail_ref[...] = rx[tt - SUBLANES:, :]
    xc = rx * cw_ref[CONV_WIDTH - 1:CONV_WIDTH, :] + cb_ref[...]
    for s in range(1, CONV_WIDTH):
        xc = xc + ext[SUBLANES - s:SUBLANES - s + tt, :] * cw_ref[CONV_WIDTH - 1 - s:CONV_WIDTH - s, :]

    xcb = xc.astype(BF16)
    r_pre = []
    i_pre = []
    for n in range(LRU_BLOCKS):
        blk = xcb[:, n * LRU_BW:(n + 1) * LRU_BW]
        r_pre.append(_dot(blk, wa_ref[n]))
        i_pre.append(_dot(blk, wx_ref[n]))
    r = jax.nn.sigmoid(jnp.concatenate(r_pre, axis=1) + ba_ref[...])
    ig = jax.nn.sigmoid(jnp.concatenate(i_pre, axis=1) + bx_ref[...])
    log_a = (-LRU_C * r) * jax.nn.softplus(-lam_ref[...])
    a = jnp.exp(log_a)
    th = jnp.tanh(log_a)
    b = jnp.sqrt(-2.0 * th / (1.0 - th)) * (ig * xc)

    a3 = a.reshape(ng, SUBLANES, LRU_WIDTH)
    b3 = b.reshape(ng, SUBLANES, LRU_WIDTH)
    sub = lax.broadcasted_iota(jnp.int32, a3.shape, 1)
    for s in (1, 2, 4):
        a_sh = pltpu.roll(a3, s, 1)
        b_sh = pltpu.roll(b3, s, 1)
        valid = sub >= s
        b3 = jnp.where(valid, a3 * b_sh, 0.0) + b3
        a3 = jnp.where(valid, a3 * a_sh, a3)
    a_ref[...] = a3
    b_ref[...] = b3

    def body(gi, carry):
        hg = a_ref[gi] * carry + b_ref[gi]
        hs_ref[gi] = hg
        return jnp.broadcast_to(hg[SUBLANES - 1:SUBLANES, :], (SUBLANES, LRU_WIDTH))

    carry_ref[...] = lax.fori_loop(0, ng, body, carry_ref[...])
    hseq = hs_ref[...].reshape(tt, LRU_WIDTH)
    y_ref[...] = (hseq * jax.nn.gelu(rg)).astype(y_ref.dtype)


def _rglru(h0, g, wrx, wrg, cw, cb, wa, ba, wx, bx, lam):
    B, TP, D = h0.shape
    nt = TP // TILE_LRU
    ng = TILE_LRU // SUBLANES
    consts = [g, wrx, wrg, cw, cb, wa, ba, wx, bx, lam]
    return pl.pallas_call(
        _rglru_kernel,
        grid=(B, nt),
        in_specs=[pl.BlockSpec((None, TILE_LRU, D), lambda b, i: (b, i, 0))]
        + [_const_spec(c.shape) for c in consts],
        out_specs=pl.BlockSpec((None, TILE_LRU, LRU_WIDTH), lambda b, i: (b, i, 0)),
        out_shape=jax.ShapeDtypeStruct((B, TP, LRU_WIDTH), BF16),
        scratch_shapes=[
            pltpu.VMEM((SUBLANES, LRU_WIDTH), F32),
            pltpu.VMEM((SUBLANES, LRU_WIDTH), F32),
            pltpu.VMEM((ng, SUBLANES, LRU_WIDTH), F32),
            pltpu.VMEM((ng, SUBLANES, LRU_WIDTH), F32),
            pltpu.VMEM((ng, SUBLANES, LRU_WIDTH), F32),
        ],
        compiler_params=pltpu.CompilerParams(
            dimension_semantics=("parallel", "arbitrary"), vmem_limit_bytes=VMEM_LIMIT),
        name="rglru",
    )(h0, *consts)


def _merge_kernel(h_ref, o_ref, y_ref, g_ref, wg_ref, bg_ref, wao_ref, wro_ref, wout_ref, g2_ref,
                  wr_ref, br_ref, h2_ref, xp_ref, idx_ref, gate_ref, rank_ref, cnt_ref, base_ref):
    i = pl.program_id(0)
    tm = h_ref.shape[0]

    @pl.when(i == 0)
    def _():
        base_ref[...] = jnp.zeros(base_ref.shape, F32)

    h = h_ref[...]
    xn = _rms(h, g_ref[...]).astype(BF16)
    gate = jax.nn.sigmoid(_dot(xn, wg_ref[...]) + bg_ref[...])
    attn = _dot(o_ref[...], wao_ref[...])
    rec = _dot(y_ref[...], wro_ref[...])
    merged = gate[:, :D_MODEL] * attn + gate[:, D_MODEL:] * rec
    h2 = h + _dot(merged.astype(BF16), wout_ref[...])
    h2_ref[...] = h2
    xn2 = _rms(h2, g2_ref[...])

    lo = pltpu.bitcast(xn2[:, :HALF].astype(BF16).astype(F32), jnp.uint32)
    hi = pltpu.bitcast(xn2[:, HALF:].astype(BF16).astype(F32), jnp.uint32)
    xp_ref[...] = (hi & jnp.uint32(0xFFFF0000)) | (lo >> 16)

    logits = jnp.dot(xn2, wr_ref[...], preferred_element_type=F32,
                     precision=lax.Precision.HIGHEST) + br_ref[...]
    lane = lax.broadcasted_iota(jnp.int32, (tm, ROUTER_PAD), 1).astype(F32)
    vals = logits
    top_v = []
    top_i = []
    hots = []
    for _ in range(TOP_K):
        m = jnp.max(vals, axis=1, keepdims=True)
        ix = jnp.min(jnp.where(vals == m, lane, float(ROUTER_PAD)), axis=1, keepdims=True)
        hot = lane == ix
        vals = jnp.where(hot, -jnp.inf, vals)
        top_v.append(m)
        top_i.append(ix)
        hots.append(hot)
    es = [jnp.exp(v - top_v[0]) for v in top_v]
    den = es[0] + es[1] + es[2] + es[3]

    multi = jnp.zeros((tm, ROUTER_PAD), F32)
    for hot in hots:
        multi = multi + jnp.where(hot, 1.0, 0.0)
    rr = lax.broadcasted_iota(jnp.int32, (tm, tm), 0)
    cc = lax.broadcasted_iota(jnp.int32, (tm, tm), 1)
    ltri = jnp.where(rr > cc, 1.0, 0.0).astype(BF16)
    before = _dot(ltri, multi.astype(BF16)) + base_ref[...]
    base_ref[...] = base_ref[...] + jnp.sum(multi, axis=0, keepdims=True)
    cnt_ref[...] = base_ref[...].astype(jnp.int32)

    idx_out = jnp.zeros((tm, ROUTER_PAD), jnp.int32)
    gate_out = jnp.zeros((tm, ROUTER_PAD), F32)
    rank_out = jnp.zeros((tm, ROUTER_PAD), jnp.int32)
    for k in range(TOP_K):
        rk = jnp.sum(jnp.where(hots[k], before, 0.0), axis=1, keepdims=True).astype(jnp.int32)
        sel = lane == k
        idx_out = jnp.where(sel, top_i[k].astype(jnp.int32), idx_out)
        gate_out = jnp.where(sel, es[k] / den, gate_out)
        rank_out = jnp.where(sel, rk, rank_out)
    idx_ref[...] = idx_out
    gate_ref[...] = gate_out
    rank_ref[...] = rank_out


def _merge(hf, of, yf, g, wg, bg, wao, wro, wout, g2, wr, br):
    NP, D = hf.shape
    nt = NP // TILE_MERGE
    consts = [g, wg, bg, wao, wro, wout, g2, wr, br]
    row = lambda w: pl.BlockSpec((TILE_MERGE, w), lambda i: (i, 0))
    return pl.pallas_call(
        _merge_kernel,
        grid=(nt,),
        in_specs=[row(D), row(of.shape[1]), row(yf.shape[1])] + [_const_spec(c.shape) for c in consts],
        out_specs=[row(D), row(HALF), row(ROUTER_PAD), row(ROUTER_PAD), row(ROUTER_PAD),
                   _const_spec((1, ROUTER_PAD))],
        out_shape=[
            jax.ShapeDtypeStruct((NP, D), F32),
            jax.ShapeDtypeStruct((NP, HALF), jnp.uint32),
            jax.ShapeDtypeStruct((NP, ROUTER_PAD), jnp.int32),
            jax.ShapeDtypeStruct((NP, ROUTER_PAD), F32),
            jax.ShapeDtypeStruct((NP, ROUTER_PAD), jnp.int32),
            jax.ShapeDtypeStruct((1, ROUTER_PAD), jnp.int32),
        ],
        scratch_shapes=[pltpu.VMEM((1, ROUTER_PAD), F32)],
        compiler_params=pltpu.CompilerParams(
            dimension_semantics=("arbitrary",), vmem_limit_bytes=VMEM_LIMIT),
        name="merge_router",
    )(hf, of, yf, *consts)


def _row_copy(src_ref, src_row, dst_ref, dst_row, sem):
    return pltpu.make_async_copy(src_ref.at[pl.ds(src_row, 1), :], dst_ref.at[pl.ds(dst_row, 1), :], sem)


def _dispatch_kernel(dest_ref, zrow_ref, zgo_ref, used_ref, x_ref, xpad_ref, zbuf_ref, sem, zsem):
    i = pl.program_id(0)
    tn = x_ref.shape[0]
    n_tiles = xpad_ref.shape[0] // TILE_EXPERT

    @pl.when(i == 0)
    def _():
        zbuf_ref[...] = jnp.zeros(zbuf_ref.shape, zbuf_ref.dtype)

        def zcopy(row):
            return pltpu.make_async_copy(
                zbuf_ref, xpad_ref.at[pl.ds(pl.multiple_of(row, TILE_EXPERT), TILE_EXPERT), :], zsem)

        for e in range(N_EXPERTS):
            @pl.when(zgo_ref[e] > 0)
            def _():
                zcopy(zrow_ref[e]).start()

        def tail_start(t, _):
            zcopy(t * TILE_EXPERT).start()
            return 0

        def tail_wait(t, _):
            zcopy(t * TILE_EXPERT).wait()
            return 0

        lax.fori_loop(used_ref[0], n_tiles, tail_start, 0)
        for e in range(N_EXPERTS):
            @pl.when(zgo_ref[e] > 0)
            def _():
                zcopy(zrow_ref[e]).wait()
        lax.fori_loop(used_ref[0], n_tiles, tail_wait, 0)

    def issue(r, _):
        for k in range(TOP_K):
            _row_copy(x_ref, r, xpad_ref, dest_ref[0, 0, r * TOP_K + k], sem).start()
        return 0

    lax.fori_loop(0, tn, issue, 0)
    for k in range(TOP_K):
        pltpu.make_async_copy(x_ref, xpad_ref.at[pl.ds(0, tn), :], sem).wait()


def _dispatch(dest, zrow, zgo, used, xp, n_rows):
    NP, W = xp.shape
    nt = NP // TILE_DISPATCH
    dest3 = dest.reshape(nt, 1, TILE_DISPATCH * TOP_K)
    return pl.pallas_call(
        _dispatch_kernel,
        grid=(nt,),
        in_specs=[
            pl.BlockSpec((1, 1, TILE_DISPATCH * TOP_K), lambda i: (i, 0, 0), memory_space=pltpu.SMEM),
            pl.BlockSpec(memory_space=pltpu.SMEM),
            pl.BlockSpec(memory_space=pltpu.SMEM),
            pl.BlockSpec(memory_space=pltpu.SMEM),
            pl.BlockSpec((TILE_DISPATCH, W), lambda i: (i, 0)),
        ],
        out_specs=pl.BlockSpec(memory_space=pl.ANY),
        out_shape=jax.ShapeDtypeStruct((n_rows, W), xp.dtype),
        scratch_shapes=[
            pltpu.VMEM((TILE_EXPERT, W), xp.dtype),
            pltpu.SemaphoreType.DMA(()),
            pltpu.SemaphoreType.DMA(()),
        ],
        compiler_params=pltpu.CompilerParams(
            dimension_semantics=("arbitrary",), vmem_limit_bytes=VMEM_LIMIT),
        name="dispatch",
    )(dest3, zrow, zgo, used, xp)


def _expert_kernel(be_ref, used_ref, x_ref, wup_ref, bup_ref, wdn_ref, bdn_ref, y_ref):
    i = pl.program_id(0)

    @pl.when(i < used_ref[0])
    def _():
        xw = x_ref[...]
        xa = pltpu.bitcast(xw << 16, F32).astype(BF16)
        xb = pltpu.bitcast(xw & jnp.uint32(0xFFFF0000), F32).astype(BF16)
        hdn = _dot(xa, wup_ref[:HALF, :]) + _dot(xb, wup_ref[HALF:, :]) + bup_ref[...]
        x_glu = jnp.minimum(hdn[:, :D_FF], SWIGLU_LIMIT)
        x_lin = jnp.clip(hdn[:, D_FF:], -SWIGLU_LIMIT, SWIGLU_LIMIT)
        act = x_glu * jax.nn.sigmoid(SWIGLU_ALPHA * x_glu) * (x_lin + 1.0)
        y_ref[...] = _dot(act.astype(BF16), wdn_ref[...]) + bdn_ref[...]

    @pl.when(i >= used_ref[0])
    def _():
        y_ref[...] = jnp.zeros(y_ref.shape, y_ref.dtype)


def _experts(block_e, used, xpad, wup, bup, wdn, bdn):
    P, W = xpad.shape
    nt = P // TILE_EXPERT
    grid_spec = pltpu.PrefetchScalarGridSpec(
        num_scalar_prefetch=2,
        grid=(nt,),
        in_specs=[
            pl.BlockSpec((TILE_EXPERT, W), lambda i, be, u: (i, 0)),
            pl.BlockSpec((None, D_MODEL, 2 * D_FF), lambda i, be, u: (be[i], 0, 0)),
            pl.BlockSpec((None, 1, 2 * D_FF), lambda i, be, u: (be[i], 0, 0)),
            pl.BlockSpec((None, D_FF, D_MODEL), lambda i, be, u: (be[i], 0, 0)),
            pl.BlockSpec((None, 1, D_MODEL), lambda i, be, u: (be[i], 0, 0)),
        ],
        out_specs=pl.BlockSpec((TILE_EXPERT, D_MODEL), lambda i, be, u: (i, 0)),
    )
    return pl.pallas_call(
        _expert_kernel,
        grid_spec=grid_spec,
        out_shape=jax.ShapeDtypeStruct((P, D_MODEL), F32),
        compiler_params=pltpu.CompilerParams(
            dimension_semantics=("arbitrary",), vmem_limit_bytes=VMEM_LIMIT),
        name="experts",
    )(block_e, used, xpad, wup, bup, wdn, bdn)


def _combine_kernel(dest_ref, gate_ref, gf_ref, h2_ref, ypad_ref, out_ref, ybuf_ref, hbuf_ref, sem, hsem,
                    *, seq, tp):
    b = pl.program_id(0)
    i = pl.program_id(1)
    tn = out_ref.shape[0]
    start = pl.multiple_of(b * tp + N_META + i * tn, SUBLANES)
    hcopy = pltpu.make_async_copy(h2_ref.at[pl.ds(start, tn), :], hbuf_ref, hsem)
    hcopy.start()

    def issue(r, _):
        for k in range(TOP_K):
            pltpu.make_async_copy(ypad_ref.at[pl.ds(dest_ref[0, 0, r * TOP_K + k], 1), :],
                                  ybuf_ref.at[k, pl.ds(r, 1), :], sem).start()
        return 0

    lax.fori_loop(0, tn, issue, 0)
    for k in range(TOP_K):
        pltpu.make_async_copy(ypad_ref.at[pl.ds(0, tn), :], ybuf_ref.at[k], sem).wait()
    hcopy.wait()

    gates = gate_ref[...]
    moe = ybuf_ref[0] * gates[:, 0:1]
    for k in range(1, TOP_K):
        moe = moe + ybuf_ref[k] * gates[:, k:k + 1]
    out_ref[...] = _rms(hbuf_ref[...] + moe, gf_ref[...])


def _combine(dest_real, gate_real, gf, h2, ypad, B, seq, tp):
    nt = seq // TILE_COMBINE
    dest3 = dest_real.reshape(B * nt, 1, TILE_COMBINE * TOP_K)
    return pl.pallas_call(
        functools.partial(_combine_kernel, seq=seq, tp=tp),
        grid=(B, nt),
        in_specs=[
            pl.BlockSpec((1, 1, TILE_COMBINE * TOP_K), lambda b, i: (b * nt + i, 0, 0),
                         memory_space=pltpu.SMEM),
            pl.BlockSpec((TILE_COMBINE, ROUTER_PAD), lambda b, i: (b * nt + i, 0)),
            _const_spec(gf.shape),
            pl.BlockSpec(memory_space=pl.ANY),
            pl.BlockSpec(memory_space=pl.ANY),
        ],
        out_specs=pl.BlockSpec((None, TILE_COMBINE, D_MODEL), lambda b, i: (b, i, 0)),
        out_shape=jax.ShapeDtypeStruct((B, seq, D_MODEL), F32),
        scratch_shapes=[
            pltpu.VMEM((TOP_K, TILE_COMBINE, D_MODEL), F32),
            pltpu.VMEM((TILE_COMBINE, D_MODEL), F32),
            pltpu.SemaphoreType.DMA(()),
            pltpu.SemaphoreType.DMA(()),
        ],
        compiler_params=pltpu.CompilerParams(
            dimension_semantics=("arbitrary", "arbitrary"), vmem_limit_bytes=VMEM_LIMIT),
        name="combine",
    )(dest3, gate_real, gf, h2, ypad)


def _rope_tables(tp, scale):
    inv = ROPE_BASE ** (-jnp.arange(0, QK_ROPE, 2, dtype=F32) / QK_ROPE)
    ang = jnp.arange(tp, dtype=F32)[:, None] * inv[None, :]
    cos, sin = jnp.cos(ang), jnp.sin(ang)
    zl = jnp.zeros((tp, QK_NOPE), F32)
    zr = jnp.zeros((tp, HEAD_PAD - QK_NOPE - QK_ROPE), F32)
    c_k = jnp.concatenate([zl, cos, cos, zr], axis=1)
    s_k = jnp.concatenate([zl, -sin, sin, zr], axis=1)
    c_q = jnp.concatenate([jnp.ones_like(zl), cos, cos, zr], axis=1) * scale
    return c_q, s_k * scale, c_k, s_k


def _swap_halves(w):
    half = QK_ROPE // 2
    return jnp.concatenate([w[..., half:], w[..., :half]], axis=-1)


def _mla_weights(w_in, w_uq, w_ukv):
    pad_r = HEAD_PAD - QK_NOPE - QK_ROPE
    wcq = w_in[:, :OFF_CKV]
    wckv = w_in[:, OFF_CKV:OFF_KR]
    w_kr = w_in[:, OFF_KR:OFF_RX]
    zl = jnp.zeros((D_MODEL, QK_NOPE), F32)
    zr = jnp.zeros((D_MODEL, pad_r), F32)
    wkr = jnp.concatenate([zl, w_kr, zr, zl, _swap_halves(w_kr), zr], axis=1)

    wq = w_uq.reshape(Q_LORA, MLA_HEADS, QK_NOPE + QK_ROPE)
    zq = jnp.zeros((Q_LORA, MLA_HEADS, pad_r), F32)
    wqa = jnp.concatenate([wq, zq], axis=2).reshape(Q_LORA, MLA_HEADS * HEAD_PAD)
    wqb = jnp.concatenate([jnp.zeros((Q_LORA, MLA_HEADS, QK_NOPE), F32), _swap_halves(wq[:, :, QK_NOPE:]), zq],
                          axis=2).reshape(Q_LORA, MLA_HEADS * HEAD_PAD)

    wkv = w_ukv.reshape(KV_LORA, MLA_HEADS, QK_NOPE + V_DIM)
    wkn = jnp.concatenate([wkv[:, :, :QK_NOPE], jnp.zeros((KV_LORA, MLA_HEADS, HEAD_PAD - QK_NOPE), F32)],
                          axis=2).reshape(KV_LORA, MLA_HEADS * HEAD_PAD)
    wv = wkv[:, :, QK_NOPE:]
    slots = []
    for h in range(MLA_HEADS):
        j = h % HEAD_GROUP
        slots.append(jnp.pad(wv[:, h, :], ((0, 0), (j * V_DIM, (HEAD_GROUP - 1 - j) * V_DIM))))
    wvx = jnp.concatenate(slots, axis=1)
    return [w.astype(BF16) for w in (wcq, wckv, wkr, wqa, wqb, wkn, wvx)]


def kernel(x, meta_tokens, norm_mix_g, w_in, b_gate, q_norm_g, w_uq, kv_norm_g, w_ukv, w_ao, conv_w, conv_b,
           w_rg_a, b_rg_a, w_rg_x, b_rg_x, rg_lambda, w_ro, w_out, norm_ffn_g, w_router, b_router, w_up, b_up,
           w_down, b_down, norm_final_g):
    B, seq, D = x.shape
    depth = w_in.shape[0]
    assert depth == 1 and D == D_MODEL
    T = N_META + seq
    TP = -(-T // T_ALIGN) * T_ALIGN
    NP = B * TP
    assert NP % TILE_MERGE == 0 and NP % TILE_DISPATCH == 0 and seq % TILE_COMBINE == 0 and TP % TILE_LRU == 0

    meta = jnp.broadcast_to(meta_tokens[None].astype(x.dtype), (B, N_META, D))
    h0 = jnp.concatenate([meta, x, jnp.zeros((B, TP - T, D), x.dtype)], axis=1)
    row = lambda v: v.reshape(1, -1)
    l = 0
    g_mix = row(norm_mix_g[l])

    scale = (QK_NOPE + QK_ROPE) ** -0.5
    cq_t, sq_t, ck_t, sk_t = _rope_tables(TP, scale)
    wcq, wckv, wkr, wqa, wqb, wkn, wvx = _mla_weights(w_in[l], w_uq[l], w_ukv[l])
    q, k, v = _mla_proj(h0, g_mix, wcq, wckv, wkr, row(q_norm_g[l]), row(kv_norm_g[l]), wqa, wqb, wkn, wvx,
                        cq_t, sq_t, ck_t, sk_t)
    o = _attention(q, k, v)

    y = _rglru(h0, g_mix, w_in[l][:, OFF_RX:OFF_RG].astype(BF16), w_in[l][:, OFF_RG:OFF_GATE].astype(BF16),
               conv_w[l], row(conv_b[l]), w_rg_a[l].astype(BF16), row(b_rg_a[l]), w_rg_x[l].astype(BF16),
               row(b_rg_x[l]), row(rg_lambda[l]))

    wr = jnp.pad(w_router[l], ((0, 0), (0, ROUTER_PAD - N_EXPERTS)))
    br = jnp.concatenate([b_router[l], jnp.full((ROUTER_PAD - N_EXPERTS,), -1e30, F32)]).reshape(1, -1)
    h2, xp, idx, gates, rank, counts = _merge(
        h0.reshape(NP, D), o.reshape(NP, -1), y.reshape(NP, -1), g_mix, w_in[l][:, OFF_GATE:].astype(BF16),
        row(b_gate[l]), w_ao[l].astype(BF16), w_ro[l].astype(BF16), w_out[l].astype(BF16),
        row(norm_ffn_g[l]), wr, br)

    counts = counts[0, :N_EXPERTS]
    padded = (counts + TILE_EXPERT - 1) // TILE_EXPERT * TILE_EXPERT
    pend = jnp.cumsum(padded)
    pstart = pend - padded
    idx4 = idx[:, :TOP_K]
    dest = pstart[idx4] + rank[:, :TOP_K]
    n_tiles = NP * TOP_K // TILE_EXPERT + N_EXPERTS
    P = n_tiles * TILE_EXPERT
    block_e = jnp.minimum(jnp.searchsorted(pend, jnp.arange(n_tiles, dtype=jnp.int32) * TILE_EXPERT, side='right'),
                          N_EXPERTS - 1).astype(jnp.int32)
    used = (pend[-1:] // TILE_EXPERT).astype(jnp.int32)
    zrow = jnp.maximum(pend - TILE_EXPERT, 0).astype(jnp.int32)
    zgo = (padded > 0).astype(jnp.int32)

    xpad = _dispatch(dest.reshape(-1).astype(jnp.int32), zrow, zgo, used, xp, P)
    ypad = _experts(block_e, used, xpad, w_up[l].astype(BF16), b_up[l].reshape(N_EXPERTS, 1, -1),
                    w_down[l].astype(BF16), b_down[l].reshape(N_EXPERTS, 1, -1))

    dest_real = dest.reshape(B, TP, TOP_K)[:, N_META:T].reshape(-1).astype(jnp.int32)
    gate_real = gates.reshape(B, TP, ROUTER_PAD)[:, N_META:T].reshape(B * seq, ROUTER_PAD)
    return _combine(dest_real, gate_real, row(norm_final_g), h2, ypad, B, seq, TP)
```
